```python
import math
import jax, jax.numpy as jnp
from jax import lax
import numpy as np

D_MODEL = 1024
BATCH = 8
SEQ = 2048
DEPTH = 2
DEC_BATCH = 128
DEC_SEQ = 8
PAST_LEN = 16384
PAGE_SIZE = 128

LRU_WIDTH = 3 * D_MODEL // 2
LRU_BLOCKS = 8
LRU_BLOCK = LRU_WIDTH // LRU_BLOCKS
LRU_CONV = 4
LRU_C = 8.0
GLA_HEADS = 4
GLA_DK = D_MODEL // (2 * GLA_HEADS)
GLA_DV = D_MODEL // GLA_HEADS
GLA_RANK = 16
GLA_TAU = 16.0
GLA_CHUNK = 64
D_FF = ((8 * D_MODEL // 3 + 127) // 128) * 128
FFN_CONV = 3
NORM_EPS = 1e-6
IN_SIZES = (LRU_WIDTH, LRU_WIDTH, GLA_HEADS * GLA_DK, GLA_HEADS * GLA_DK, GLA_HEADS * GLA_DV,
            GLA_HEADS * GLA_DV, GLA_RANK, D_MODEL, D_MODEL)
N_IN = sum(IN_SIZES)

kernel_name = "hybrid_rglru_gla_convffn_adaln_step"


def _rmsnorm(x, g):
    x32 = x.astype(jnp.float32)
    r = lax.rsqrt(jnp.mean(x32 * x32, axis=-1, keepdims=True) + NORM_EPS)
    return (x32 * r * g.astype(jnp.float32)).astype(x.dtype)


def _split_in(z):
    idx, acc = [], 0
    for s in IN_SIZES[:-1]:
        acc += s
        idx.append(acc)
    return jnp.split(z, idx, axis=-1)


def _causal_conv(x, buf, w, b):
    K = w.shape[0]
    T = x.shape[1]
    xp = jnp.concatenate([buf.astype(x.dtype), x], axis=1)
    y = b + sum(xp[:, j:j + T] * w[j] for j in range(K))
    return y, xp[:, xp.shape[1] - (K - 1):]


def _lin_combine(e1, e2):
    a1, b1 = e1
    a2, b2 = e2
    return a1 * a2, a2 * b1 + b2


def _rg_lru(x, h0, wa, ba, wx, bx, lam, reset_first):
    B, T, W = x.shape
    x32 = x.astype(jnp.float32)
    xb = x32.reshape(B, T, LRU_BLOCKS, LRU_BLOCK)
    r = jax.nn.sigmoid(jnp.einsum('btnc,ncd->btnd', xb, wa.astype(jnp.float32)).reshape(B, T, W) + ba)
    i = jax.nn.sigmoid(jnp.einsum('btnc,ncd->btnd', xb, wx.astype(jnp.float32)).reshape(B, T, W) + bx)
    log_a = -LRU_C * r * jax.nn.softplus(-lam.astype(jnp.float32))
    a = jnp.exp(log_a)
    mult = jnp.sqrt(-jnp.expm1(2.0 * log_a))
    if reset_first:
        mult = mult.at[:, 0].set(1.0)
    A, Bc = lax.associative_scan(_lin_combine, (a, mult * i * x32), axis=1)
    h = A * h0.astype(jnp.float32)[:, None] + Bc
    return h, h[:, -1]


def _gla(q, k, v, log_a, S0):
    B, T, H, _ = q.shape
    C = math.gcd(T, GLA_CHUNK)
    n = T // C

    def chunks(t):
        return t.astype(jnp.float32).reshape(B, n, C, H, -1).transpose(1, 0, 3, 2, 4)

    q, k, v, la = chunks(q), chunks(k), chunks(v), chunks(log_a)
    b = jnp.cumsum(la, axis=3)
    b_last = b[:, :, :, -1:]
    qd = q * jnp.exp(b) * (GLA_DK ** -0.5)
    kd = k * jnp.exp(-b)
    ke = k * jnp.exp(b_last - b)
    mask = jnp.tril(jnp.ones((C, C), dtype=bool))
    A = jnp.where(mask, jnp.einsum('nbhtk,nbhsk->nbhts', qd, kd), 0.0)
    o_intra = jnp.einsum('nbhts,nbhsv->nbhtv', A, v)

    def step(S, inp):
        qd_c, ke_c, v_c, bl_c = inp
        o = jnp.einsum('bhtk,bhkv->bhtv', qd_c, S)
        S = jnp.exp(bl_c[:, :, 0, :])[..., None] * S + jnp.einsum('bhsk,bhsv->bhkv', ke_c, v_c)
        return S, o

    S_fin, o_inter = lax.scan(step, S0.astype(jnp.float32), (qd, ke, v, b_last))
    o = (o_intra + o_inter).transpose(1, 0, 3, 2, 4).reshape(B, T, H, GLA_DV)
    return o, S_fin


def _mixer(h, conv_buf, h0, S0, p, reset_first):
    B, T, _ = h.shape
    z = h @ p['w_in']
    xl, gl, q, k, v, og, lr, mga, mgb = _split_in(z)
    xc, new_conv = _causal_conv(xl, conv_buf, p['lru_conv_w'], p['lru_conv_b'])
    hl, h_last = _rg_lru(xc, h0, p['lru_wa'], p['lru_ba'], p['lru_wx'], p['lru_bx'], p['lru_lambda'], reset_first)
    yA = hl.astype(h.dtype) * jax.nn.gelu(gl)
    logit = (lr @ p['gla_w_lr'] + p['gla_b_lr']).astype(jnp.float32)
    log_alpha = (jax.nn.log_sigmoid(logit) / GLA_TAU).reshape(B, T, GLA_HEADS, GLA_DK)
    o, S_fin = _gla(q.reshape(B, T, GLA_HEADS, GLA_DK), k.reshape(B, T, GLA_HEADS, GLA_DK),
                    v.reshape(B, T, GLA_HEADS, GLA_DV), log_alpha, S0)
    o = _rmsnorm(o, p['gla_norm_g']).astype(h.dtype)
    yB = (o * jax.nn.silu(og.reshape(B, T, GLA_HEADS, GLA_DV))).reshape(B, T, GLA_HEADS * GLA_DV)
    merged = jax.nn.sigmoid(mga) * (yA @ p['w_branch_a']) + jax.nn.sigmoid(mgb) * (yB @ p['w_branch_b'])
    return merged @ p['w_out'], new_conv, h_last, S_fin


def _ffn(h, ffn_buf, p):
    u = h @ p['ffn_w_up']
    ug, uv = jnp.split(u, [D_FF], axis=-1)
    ugc, new_buf = _causal_conv(ug, ffn_buf, p['ffn_conv_w'], p['ffn_conv_b'])
    return (jax.nn.silu(ugc) * uv) @ p['ffn_w_down'], new_buf


def _layer(x, c, h0, conv_buf, S0, ffn_buf, p, reset_first):
    mod = (jax.nn.silu(c) @ p['ada_w'] + p['ada_b'])[:, None, :]
    sh1, sc1, g1, sh2, sc2, g2 = jnp.split(mod, 6, axis=-1)
    h = _rmsnorm(x, p['norm_mix_g']) * (1 + sc1) + sh1
    m, new_conv, h_last, S_fin = _mixer(h, conv_buf, h0, S0, p, reset_first)
    x = x + g1 * m
    h = _rmsnorm(x, p['norm_ffn_g']) * (1 + sc2) + sh2
    f, new_ffn = _ffn(h, ffn_buf, p)
    x = x + g2 * f
    return x, h_last.astype(h0.dtype), new_conv.astype(conv_buf.dtype), S_fin.astype(S0.dtype), new_ffn.astype(ffn_buf.dtype)


def setup_inputs(seed: int = 0) -> dict:
    key = jax.random.key(seed)
    ks = iter(jax.random.split(key, 40))
    f32 = jnp.float32

    def nrm(shape, scale=1.0):
        return jax.random.normal(next(ks), shape, f32) * scale

    D = D_MODEL
    a0 = jax.random.uniform(next(ks), (DEPTH, LRU_WIDTH), f32, 0.9, 0.999)
    s = a0 ** (1.0 / LRU_C)
    lam = jnp.log(s) - jnp.log1p(-s)
    return {
        'x_prompt': nrm((BATCH, SEQ, D)),
        'x_sample': nrm((DEC_BATCH, DEC_SEQ, D)),
        'c_prompt': nrm((BATCH, D)),
        'c_sample': nrm((DEC_BATCH, D)),
        'state_lru_h': nrm((DEPTH, DEC_BATCH, LRU_WIDTH)),
        'state_lru_conv': nrm((DEPTH, DEC_BATCH, LRU_CONV - 1, LRU_WIDTH)),
        'state_gla': nrm((DEPTH, DEC_BATCH, GLA_HEADS, GLA_DK, GLA_DV)),
        'state_ffn_conv': nrm((DEPTH, DEC_BATCH, FFN_CONV - 1, D_FF)),
        'ada_w': nrm((DEPTH, D, 6 * D), 0.3 * D ** -0.5),
        'ada_b': nrm((DEPTH, 6 * D), 0.01),
        'norm_mix_g': 1.0 + nrm((DEPTH, D), 0.02),
        'norm_ffn_g': 1.0 + nrm((DEPTH, D), 0.02),
        'w_in': nrm((DEPTH, D, N_IN), D ** -0.5),
        'lru_conv_w': nrm((DEPTH, LRU_CONV, LRU_WIDTH), LRU_CONV ** -0.5),
        'lru_conv_b': nrm((DEPTH, LRU_WIDTH), 0.01),
        'lru_wa': nrm((DEPTH, LRU_BLOCKS, LRU_BLOCK, LRU_BLOCK), LRU_BLOCK ** -0.5),
        'lru_ba': nrm((DEPTH, LRU_WIDTH), 0.01),
        'lru_wx': nrm((DEPTH, LRU_BLOCKS, LRU_BLOCK, LRU_BLOCK), LRU_BLOCK ** -0.5),
        'lru_bx': nrm((DEPTH, LRU_WIDTH), 0.01),
        'lru_lambda': lam,
        'gla_w_lr': nrm((DEPTH, GLA_RANK, GLA_HEADS * GLA_DK), GLA_RANK ** -0.5),
        'gla_b_lr': nrm((DEPTH, GLA_HEADS * GLA_DK), 0.01),
        'gla_norm_g': 1.0 + nrm((DEPTH, GLA_DV), 0.02),
        'w_branch_a': nrm((DEPTH, LRU_WIDTH, D), LRU_WIDTH ** -0.5),
        'w_branch_b': nrm((DEPTH, GLA_HEADS * GLA_DV, D), (GLA_HEADS * GLA_DV) ** -0.5),
        'w_out': nrm((DEPTH, D, D), D ** -0.5),
        'ffn_w_up': nrm((DEPTH, D, 2 * D_FF), D ** -0.5),
        'ffn_conv_w': nrm((DEPTH, FFN_CONV, D_FF), FFN_CONV ** -0.5),
        'ffn_conv_b': nrm((DEPTH, D_FF), 0.01),
        'ffn_w_down': nrm((DEPTH, D_FF, D), D_FF ** -0.5),
        'final_norm_g': 1.0 + nrm((D,), 0.02),
    }


def reference(x_prompt, x_sample, c_prompt, c_sample, state_lru_h, state_lru_conv, state_gla, state_ffn_conv,
              ada_w, ada_b, norm_mix_g, norm_ffn_g, w_in, lru_conv_w, lru_conv_b, lru_wa, lru_ba, lru_wx, lru_bx,
              lru_lambda, gla_w_lr, gla_b_lr, gla_norm_g, w_branch_a, w_branch_b, w_out, ffn_w_up, ffn_conv_w,
              ffn_conv_b, ffn_w_down, final_norm_g):
    xp, xs = x_prompt, x_sample
    dt = x_prompt.dtype
    hp_l, hs_l, cp_l, cs_l, sp_l, ss_l, fp_l, fs_l = [], [], [], [], [], [], [], []
    for l in range(DEPTH):
        p = {'ada_w': ada_w[l], 'ada_b': ada_b[l], 'norm_mix_g': norm_mix_g[l], 'norm_ffn_g': norm_ffn_g[l],
             'w_in': w_in[l], 'lru_conv_w': lru_conv_w[l], 'lru_conv_b': lru_conv_b[l], 'lru_wa': lru_wa[l],
             'lru_ba': lru_ba[l], 'lru_wx': lru_wx[l], 'lru_bx': lru_bx[l], 'lru_lambda': lru_lambda[l],
             'gla_w_lr': gla_w_lr[l], 'gla_b_lr': gla_b_lr[l], 'gla_norm_g': gla_norm_g[l],
             'w_branch_a': w_branch_a[l], 'w_branch_b': w_branch_b[l], 'w_out': w_out[l],
             'ffn_w_up': ffn_w_up[l], 'ffn_conv_w': ffn_conv_w[l], 'ffn_conv_b': ffn_conv_b[l],
             'ffn_w_down': ffn_w_down[l]}
        Bp = xp.shape[0]
        xp, hp, cp, sp, fp = _layer(
            xp, c_prompt,
            jnp.zeros((Bp, LRU_WIDTH), dt), jnp.zeros((Bp, LRU_CONV - 1, LRU_WIDTH), dt),
            jnp.zeros((Bp, GLA_HEADS, GLA_DK, GLA_DV), dt), jnp.zeros((Bp, FFN_CONV - 1, D_FF), dt),
            p, True)
        xs, hs, cs, ss, fs = _layer(xs, c_sample, state_lru_h[l], state_lru_conv[l], state_gla[l],
                                    state_ffn_conv[l], p, False)
        hp_l.append(hp); hs_l.append(hs); cp_l.append(cp); cs_l.append(cs)
        sp_l.append(sp); ss_l.append(ss); fp_l.append(fp); fs_l.append(fs)
    y_prompt = _rmsnorm(xp, final_norm_g)
    y_sample = _rmsnorm(xs, final_norm_g)
    return (y_prompt, y_sample,
            jnp.stack(hp_l), jnp.stack(hs_l),
            jnp.stack(cp_l), jnp.stack(cs_l),
            jnp.stack(sp_l), jnp.stack(ss_l),
            jnp.stack(fp_l), jnp.stack(fs_l))
```

```python
import functools

import jax
import jax.numpy as jnp
from jax import lax
from jax.experimental import pallas as pl
from jax.experimental.pallas import tpu as pltpu

F32 = jnp.float32
BF16 = jnp.bfloat16

D_MODEL = 1024
DEPTH = 2
LRU_WIDTH = 1536
LRU_BLOCKS = 8
LRU_BLOCK = LRU_WIDTH // LRU_BLOCKS
LRU_PAIR = 2 * LRU_BLOCK
LRU_CONV = 4
LRU_C = 8.0
GLA_HEADS = 4
GLA_DK = 128
GLA_DV = 256
GLA_RANK = 16
GLA_TAU = 16.0
GLA_CHUNK = 64
D_FF = 2816
FFN_CONV = 3
FFN_COLS = 256
NORM_EPS = 1e-6
N_LRU = 2 * LRU_WIDTH
N_REST = 5376
LR_COL = 5120
MIB = 2 ** 20


def _params(semantics, vmem_mib):
    return pltpu.CompilerParams(dimension_semantics=semantics, vmem_limit_bytes=vmem_mib * MIB)


def _dot(a, b):
    return jnp.dot(a, b, preferred_element_type=F32)


def _dot_tn(a, b):
    return lax.dot_general(a, b, (((0,), (0,)), ((), ())), preferred_element_type=F32)


def _dot_nt(a, b):
    return lax.dot_general(a, b, (((1,), (1,)), ((), ())), preferred_element_type=F32)


def _softplus(x):
    return jnp.maximum(x, 0.0) + jnp.log1p(jnp.exp(-jnp.abs(x)))


def _silu(x):
    return x * jax.nn.sigmoid(x)


def _rms(x, g):
    r = lax.rsqrt(jnp.mean(x * x, axis=-1, keepdims=True) + NORM_EPS)
    return x * r * g


def _ada_kernel(c_ref, w_ref, b_ref, o_ref):
    s = _silu(c_ref[...]).astype(BF16)
    o_ref[...] = _dot(s, w_ref[...].astype(BF16)) + b_ref[...]


def _ada_mod(c_all, ada_w, ada_b):
    nb = c_all.shape[0]
    tn = 1536
    return pl.pallas_call(
        _ada_kernel,
        grid=(DEPTH, 6 * D_MODEL // tn),
        in_specs=[
            pl.BlockSpec((nb, D_MODEL), lambda l, j: (0, 0)),
            pl.BlockSpec((None, D_MODEL, tn), lambda l, j: (l, 0, j)),
            pl.BlockSpec((None, 1, tn), lambda l, j: (l, 0, j)),
        ],
        out_specs=pl.BlockSpec((None, nb, tn), lambda l, j: (l, 0, j)),
        out_shape=jax.ShapeDtypeStruct((DEPTH, nb, 6 * D_MODEL), F32),
        compiler_params=_params(("parallel", "parallel"), 40),
        name="ada_mod",
    )(c_all, ada_w, ada_b.reshape(DEPTH, 1, 6 * D_MODEL))


class _Layout:
    def __init__(self, batch_major, dims, blk):
        self.batch_major = batch_major
        self.dims = dims
        self.blk = blk
        self.grid = (dims[0] // blk[0], dims[1] // blk[1])
        self.rows = blk[0] * blk[1]

    def tok(self, width, col=0):
        return pl.BlockSpec((self.blk[0], self.blk[1], width), lambda i, j, c=col: (i, j, c))

    def mod(self, k):
        if self.batch_major:
            return pl.BlockSpec((self.blk[0], 1, D_MODEL), lambda i, j, k=k: (i, 0, k))
        return pl.BlockSpec((1, self.blk[1], D_MODEL), lambda i, j, k=k: (0, j, k))


def _const(shape):
    nd = len(shape)
    return pl.BlockSpec(shape, lambda *_: (0,) * nd)


def _weight(shape):
    nd = len(shape)
    return pl.BlockSpec(shape, lambda *_: (0,) * nd, pipeline_mode=pl.Buffered(1))


def _norm_mod_kernel(x_ref, g_ref, sc_ref, sh_ref, o_ref):
    h = _rms(x_ref[...], g_ref[...]) * (1.0 + sc_ref[...]) + sh_ref[...]
    o_ref[...] = h.astype(o_ref.dtype)


def _norm_mod(lay, x, g, mod, k_shift, k_scale):
    return pl.pallas_call(
        _norm_mod_kernel,
        grid=lay.grid,
        in_specs=[lay.tok(D_MODEL), _const((1, 1, D_MODEL)), lay.mod(k_scale), lay.mod(k_shift)],
        out_specs=lay.tok(D_MODEL),
        out_shape=jax.ShapeDtypeStruct(x.shape, BF16),
        compiler_params=_params(("parallel", "parallel"), 32),
        name="norm_mod",
    )(x, g.reshape(1, 1, D_MODEL), mod, mod)


def _matmul_kernel(h_ref, w_ref, o_ref):
    o_ref[...] = _dot(h_ref[...], w_ref[...])


def _matmul(h2d, w, tn):
    rows, kdim = h2d.shape
    n = w.shape[1]
    tm = min(rows, 1024)
    return pl.pallas_call(
        _matmul_kernel,
        grid=(n // tn, rows // tm),
        in_specs=[pl.BlockSpec((tm, kdim), lambda j, i: (i, 0)),
                  pl.BlockSpec((kdim, tn), lambda j, i: (0, j))],
        out_specs=pl.BlockSpec((tm, tn), lambda j, i: (i, j)),
        out_shape=jax.ShapeDtypeStruct((rows, n), F32),
        compiler_params=_params(("parallel", "parallel"), 48),
        name="in_proj",
    )(h2d, w)


def _lru_kernel(xl_ref, gl_ref, conv0_ref, h0_ref, cw_ref, cb_ref, wg_ref, ba_ref, bx_ref, lam_ref,
                y_ref, convn_ref, hlast_ref, xbuf, abuf, bbuf, hcar, *, tt, bb, reset_first, lane_chunk):
    it = pl.program_id(1)
    rows = tt * bb
    halo = (LRU_CONV - 1) * bb

    @pl.when(it == 0)
    def _():
        xbuf[0:halo, :] = conv0_ref[...].reshape(halo, LRU_WIDTH)
        hcar[...] = h0_ref[...]

    xbuf[halo:halo + rows, :] = xl_ref[...].reshape(rows, LRU_WIDTH)

    for p in range(LRU_WIDTH // LRU_PAIR):
        cs = slice(p * LRU_PAIR, (p + 1) * LRU_PAIR)
        xc = cb_ref[:, cs] + sum(xbuf[j * bb:j * bb + rows, cs] * cw_ref[j:j + 1, cs] for j in range(LRU_CONV))
        gates = _dot(xc.astype(BF16), wg_ref[p])
        r = jax.nn.sigmoid(gates[:, :LRU_PAIR] + ba_ref[:, cs])
        i = jax.nn.sigmoid(gates[:, LRU_PAIR:] + bx_ref[:, cs])
        log_a = -LRU_C * r * _softplus(-lam_ref[:, cs])
        a = jnp.exp(log_a)
        mult = jnp.sqrt(jnp.tanh(-log_a) * (a * a + 1.0))
        if reset_first:
            first = (lax.broadcasted_iota(jnp.int32, (rows, LRU_PAIR), 0) < bb) & (it == 0)
            mult = jnp.where(first, 1.0, mult)
        abuf[:, cs] = a
        bbuf[:, cs] = mult * i * xc

    for lc in range(LRU_WIDTH // lane_chunk):
        cl = slice(lc * lane_chunk, (lc + 1) * lane_chunk)

        def step(t, h, cl=cl):
            rs = pl.ds(pl.multiple_of(t * bb, bb), bb)
            h = abuf[rs, cl] * h + bbuf[rs, cl]
            bbuf[rs, cl] = h
            return h

        hcar[:, cl] = lax.fori_loop(0, tt, step, hcar[:, cl])

    y = bbuf[...] * jax.nn.gelu(gl_ref[...].reshape(rows, LRU_WIDTH))
    y_ref[...] = y.reshape(tt, bb, LRU_WIDTH).astype(y_ref.dtype)

    tail = xbuf[rows:rows + halo, :]
    xbuf[0:halo, :] = tail

    @pl.when(it == pl.num_programs(1) - 1)
    def _():
        convn_ref[...] = tail.reshape(LRU_CONV - 1, bb, LRU_WIDTH)
        hlast_ref[...] = hcar[...]


def _lru(z_lru, conv0, h0, w, *, tt, bb, reset_first):
    t_len, b_len, _ = z_lru.shape
    rows = tt * bb
    lane_chunk = min(LRU_WIDTH, max(128, (16 * 1024 // bb) // 128 * 128))
    while LRU_WIDTH % lane_chunk:
        lane_chunk -= 128
    kern = functools.partial(_lru_kernel, tt=tt, bb=bb, reset_first=reset_first, lane_chunk=lane_chunk)
    tok = lambda c: pl.BlockSpec((tt, bb, LRU_WIDTH), lambda i, j, c=c: (j, i, c))
    return pl.pallas_call(
        kern,
        grid=(b_len // bb, t_len // tt),
        in_specs=[tok(0), tok(1),
                  pl.BlockSpec((LRU_CONV - 1, bb, LRU_WIDTH), lambda i, j: (0, i, 0)),
                  pl.BlockSpec((bb, LRU_WIDTH), lambda i, j: (i, 0)),
                  _const((LRU_CONV, LRU_WIDTH)), _const((1, LRU_WIDTH)),
                  _const((LRU_WIDTH // LRU_PAIR, LRU_PAIR, 2 * LRU_PAIR)),
                  _const((1, LRU_WIDTH)), _const((1, LRU_WIDTH)), _const((1, LRU_WIDTH))],
        out_specs=[tok(0),
                   pl.BlockSpec((LRU_CONV - 1, bb, LRU_WIDTH), lambda i, j: (0, i, 0)),
                   pl.BlockSpec((bb, LRU_WIDTH), lambda i, j: (i, 0))],
        out_shape=[jax.ShapeDtypeStruct((t_len, b_len, LRU_WIDTH), BF16),
                   jax.ShapeDtypeStruct((LRU_CONV - 1, b_len, LRU_WIDTH), F32),
                   jax.ShapeDtypeStruct((b_len, LRU_WIDTH), F32)],
        scratch_shapes=[pltpu.VMEM((rows + (LRU_CONV - 1) * bb, LRU_WIDTH), F32),
                        pltpu.VMEM((rows, LRU_WIDTH), F32),
                        pltpu.VMEM((rows, LRU_WIDTH), F32),
                        pltpu.VMEM((bb, LRU_WIDTH), F32)],
        compiler_params=_params(("parallel", "arbitrary"), 48),
        name="rg_lru",
    )(z_lru, z_lru, conv0, h0, w["lru_conv_w"], w["lru_conv_b"], w["lru_wg"], w["lru_ba"], w["lru_bx"],
      w["lru_lambda"])


def _split3(x, dt):
    hi = x.astype(BF16).astype(F32)
    r1 = x - hi
    mid = r1.astype(BF16).astype(F32)
    lo = (r1 - mid).astype(BF16).astype(F32)
    return hi.astype(dt), mid.astype(dt), lo.astype(dt)


def _gla_kernel(q_ref, k_ref, v_ref, og_ref, lr_ref, s0_ref, wlr_ref, blr_ref, gn_ref,
                y_ref, s_ref, la_buf, *, bb, tt, chunk, mm):
    it = pl.program_id(1)
    n_chunks = tt // chunk

    @pl.when(it == 0)
    def _():
        s_ref[...] = s0_ref[...]

    lr = lr_ref[...].reshape(bb * tt, 128).astype(BF16)
    logit = _dot(lr, wlr_ref[...]) + blr_ref[...]
    la_buf[...] = -_softplus(-logit) * (1.0 / GLA_TAU)

    row = lax.broadcasted_iota(jnp.int32, (chunk, chunk), 0)
    col = lax.broadcasted_iota(jnp.int32, (chunk, chunk), 1)
    tril = col <= row
    tri = tril.astype(mm)
    ones = jnp.ones((chunk, GLA_DK), mm)
    gn = gn_ref[...]

    def body(idx, carry):
        s = idx // n_chunks
        r0 = pl.multiple_of((idx % n_chunks) * chunk, chunk)
        rs = pl.ds(r0, chunk)
        la = la_buf[pl.ds(pl.multiple_of(s * tt + r0, chunk), chunk), :]
        parts = _split3(la, mm)
        bcum = sum(_dot(tri, p) for p in parts)
        dec = jnp.exp(sum(_dot_tn(p, ones) for p in parts))
        q = q_ref[s, rs, :]
        k = k_ref[s, rs, :]
        vb = v_ref[s, rs, :].astype(mm)
        b_last = bcum[chunk - 1:chunk, :]
        qd = (q * jnp.exp(bcum) * (GLA_DK ** -0.5)).astype(mm)
        kd = (k * jnp.exp(-bcum)).astype(mm)
        ke = (k * jnp.exp(b_last - bcum)).astype(mm)
        for h in range(GLA_HEADS):
            ks = slice(h * GLA_DK, (h + 1) * GLA_DK)
            vs = slice(h * GLA_DV, (h + 1) * GLA_DV)
            att = jnp.where(tril, _dot_nt(qd[:, ks], kd[:, ks]), 0.0)
            state = s_ref[s, h]
            o = _dot(att.astype(mm), vb[:, vs]) + _dot(qd[:, ks], state.astype(mm))
            dh = dec[ks, :]
            s_ref[s, h] = jnp.concatenate([dh, dh], axis=1) * state + _dot_tn(ke[:, ks], vb[:, vs])
            y = _rms(o, gn) * _silu(og_ref[s, rs, vs])
            y_ref[s, rs, vs] = y.astype(y_ref.dtype)
        return carry

    lax.fori_loop(0, bb * n_chunks, body, 0)


def _gla(z_rest, s0, w, *, bb, tt, out_dtype, mm):
    b_len, t_len, _ = z_rest.shape
    chunk = min(GLA_CHUNK, t_len)
    hk = GLA_HEADS * GLA_DK
    hv = GLA_HEADS * GLA_DV
    kern = functools.partial(_gla_kernel, bb=bb, tt=tt, chunk=chunk, mm=mm)
    tok = lambda width, c: pl.BlockSpec((bb, tt, width), lambda i, j, c=c: (i, j, c))
    st = pl.BlockSpec((bb, GLA_HEADS, GLA_DK, GLA_DV), lambda i, j: (i, 0, 0, 0))
    return pl.pallas_call(
        kern,
        grid=(b_len // bb, t_len // tt),
        in_specs=[tok(hk, 0), tok(hk, 1), tok(hv, 1), tok(hv, 2), tok(128, LR_COL // 128), st,
                  _const((128, hk)), _const((1, hk)), _const((1, GLA_DV))],
        out_specs=[tok(hv, 0), st],
        out_shape=[jax.ShapeDtypeStruct((b_len, t_len, hv), out_dtype),
                   jax.ShapeDtypeStruct(s0.shape, F32)],
        scratch_shapes=[pltpu.VMEM((bb * tt, hk), F32)],
        compiler_params=_params(("parallel", "arbitrary"), 48),
        name="gla",
    )(z_rest, z_rest, z_rest, z_rest, z_rest, s0, w["gla_w_lr"], w["gla_b_lr"], w["gla_norm_g"])


def _merge_kernel(ya_ref, yb_ref, mga_ref, mgb_ref, x_ref, g1_ref, sc_ref, sh_ref, gf_ref,
                  wa_ref, wb_ref, wo_ref, xo_ref, ho_ref, *, rows):
    pa = _dot(ya_ref[...].reshape(rows, LRU_WIDTH).astype(BF16), wa_ref[...])
    pb = _dot(yb_ref[...].reshape(rows, D_MODEL).astype(BF16), wb_ref[...])
    m = (jax.nn.sigmoid(mga_ref[...].reshape(rows, D_MODEL)) * pa
         + jax.nn.sigmoid(mgb_ref[...].reshape(rows, D_MODEL)) * pb)
    out = _dot(m.astype(BF16), wo_ref[...])
    xn = x_ref[...] + g1_ref[...] * out.reshape(x_ref.shape)
    xo_ref[...] = xn
    ho_ref[...] = (_rms(xn, gf_ref[...]) * (1.0 + sc_ref[...]) + sh_ref[...]).astype(ho_ref.dtype)


def _merge(lay, ya, yb, z_rest, x, mod, w):
    kern = functools.partial(_merge_kernel, rows=lay.rows)
    return pl.pallas_call(
        kern,
        grid=lay.grid,
        in_specs=[lay.tok(LRU_WIDTH), lay.tok(D_MODEL), lay.tok(D_MODEL, 3), lay.tok(D_MODEL, 4),
                  lay.tok(D_MODEL), lay.mod(2), lay.mod(4), lay.mod(3), _const((1, 1, D_MODEL)),
                  _weight((LRU_WIDTH, D_MODEL)), _weight((D_MODEL, D_MODEL)), _weight((D_MODEL, D_MODEL))],
        out_specs=[lay.tok(D_MODEL), lay.tok(D_MODEL)],
        out_shape=[jax.ShapeDtypeStruct(x.shape, F32), jax.ShapeDtypeStruct(x.shape, BF16)],
        compiler_params=_params(("parallel", "parallel"), 48),
        name="merge",
    )(ya, yb, z_rest, z_rest, x, mod, mod, mod, w["norm_ffn_g"], w["w_branch_a"], w["w_branch_b"], w["w_out"])


def _ffn_kernel(h_ref, x_ref, st_ref, g2_ref, sc_ref, sh_ref, gn_ref, wug_ref, wuv_ref, cw_ref, cb_ref, wd_ref,
                xo_ref, no_ref, sto_ref, prev, wbuf, *, rows, step, pad, carried):
    it = pl.program_id(1)
    halo = (FFN_CONV - 1) * step

    def init():
        prev[...] = st_ref[...].reshape(halo, D_FF)

    if carried:
        pl.when(it == 0)(init)
    else:
        init()

    h = h_ref[...].reshape(rows, D_MODEL)
    acc = None
    for j in range(D_FF // FFN_COLS):
        cs = slice(j * FFN_COLS, (j + 1) * FFN_COLS)
        ug = _dot(h, wug_ref[:, cs])
        uv = _dot(h, wuv_ref[:, cs])
        wbuf[pad - halo:pad, :] = prev[:, cs]
        wbuf[pad:pad + rows, :] = ug
        ugc = (cb_ref[:, cs] + wbuf[pad - 2 * step:pad - 2 * step + rows, :] * cw_ref[0:1, cs]
               + wbuf[pad - step:pad - step + rows, :] * cw_ref[1:2, cs] + ug * cw_ref[2:3, cs])
        prev[:, cs] = wbuf[pad + rows - halo:pad + rows, :]
        part = _dot((_silu(ugc) * uv).astype(BF16), wd_ref[cs, :])
        acc = part if acc is None else acc + part

    xn = x_ref[...] + g2_ref[...] * acc.reshape(x_ref.shape)
    xo_ref[...] = xn
    no_ref[...] = (_rms(xn, gn_ref[...]) * (1.0 + sc_ref[...]) + sh_ref[...]).astype(no_ref.dtype)

    def fin():
        sto_ref[...] = prev[...].reshape(sto_ref.shape)

    if carried:
        pl.when(it == pl.num_programs(1) - 1)(fin)
    else:
        fin()


def _ffn(lay, h, x, state, mod, mod_next, k_shift, k_scale, gn, w, out_dtype):
    step = 1 if lay.batch_major else lay.blk[1]
    halo = (FFN_CONV - 1) * step
    pad = -(-halo // 8) * 8
    if lay.batch_major:
        st_spec = pl.BlockSpec((lay.blk[0], FFN_CONV - 1, D_FF), lambda i, j: (i, 0, 0))
    else:
        st_spec = pl.BlockSpec((FFN_CONV - 1, lay.blk[1], D_FF), lambda i, j: (0, j, 0))
    kern = functools.partial(_ffn_kernel, rows=lay.rows, step=step, pad=pad, carried=lay.batch_major)
    return pl.pallas_call(
        kern,
        grid=lay.grid,
        in_specs=[lay.tok(D_MODEL), lay.tok(D_MODEL), st_spec, lay.mod(5), lay.mod(k_scale), lay.mod(k_shift),
                  _const((1, 1, D_MODEL)),
                  _weight((D_MODEL, D_FF)), _weight((D_MODEL, D_FF)), _const((FFN_CONV, D_FF)), _const((1, D_FF)),
                  _weight((D_FF, D_MODEL))],
        out_specs=[lay.tok(D_MODEL), lay.tok(D_MODEL), st_spec],
        out_shape=[jax.ShapeDtypeStruct(x.shape, F32), jax.ShapeDtypeStruct(x.shape, out_dtype),
                   jax.ShapeDtypeStruct(state.shape, F32)],
        scratch_shapes=[pltpu.VMEM((halo, D_FF), F32), pltpu.VMEM((pad + lay.rows, FFN_COLS), F32)],
        compiler_params=_params(("parallel", "arbitrary"), 56),
        name="conv_ffn",
    )(h, x, state, mod, mod_next, mod_next, gn, w["ffn_w_up_g"], w["ffn_w_up_v"], w["ffn_conv_w"], w["ffn_conv_b"],
      w["ffn_w_down"])


def _layer_weights(l, p):
    w_in = p["w_in"][l]
    off = N_LRU + 2 * GLA_HEADS * GLA_DK + 2 * GLA_HEADS * GLA_DV
    w_rest = jnp.concatenate(
        [w_in[:, N_LRU:off], w_in[:, off + GLA_RANK:], w_in[:, off:off + GLA_RANK],
         jnp.zeros((D_MODEL, N_REST - (LR_COL + GLA_RANK)), F32)], axis=1)
    wa = p["lru_wa"][l].reshape(LRU_BLOCKS // 2, 2, LRU_BLOCK, LRU_BLOCK)
    wx = p["lru_wx"][l].reshape(LRU_BLOCKS // 2, 2, LRU_BLOCK, LRU_BLOCK)
    zero = jnp.zeros_like(wa[:, 0])

    def pair(wb):
        return jnp.concatenate([jnp.concatenate([wb[:, 0], zero], axis=2),
                                jnp.concatenate([zero, wb[:, 1]], axis=2)], axis=1)

    wg = jnp.concatenate([pair(wa), pair(wx)], axis=2)
    w_lr = jnp.concatenate([p["gla_w_lr"][l], jnp.zeros((128 - GLA_RANK, GLA_HEADS * GLA_DK), F32)], axis=0)
    row = lambda a: a.reshape(1, -1)
    return {
        "w_lru": w_in[:, :N_LRU].astype(BF16), "w_rest": w_rest.astype(BF16),
        "lru_conv_w": p["lru_conv_w"][l], "lru_conv_b": row(p["lru_conv_b"][l]), "lru_wg": wg.astype(BF16),
        "lru_ba": row(p["lru_ba"][l]), "lru_bx": row(p["lru_bx"][l]), "lru_lambda": row(p["lru_lambda"][l]),
        "gla_w_lr": w_lr.astype(BF16), "gla_b_lr": row(p["gla_b_lr"][l]), "gla_norm_g": row(p["gla_norm_g"][l]),
        "w_branch_a": p["w_branch_a"][l].astype(BF16), "w_branch_b": p["w_branch_b"][l].astype(BF16),
        "w_out": p["w_out"][l].astype(BF16), "norm_ffn_g": p["norm_ffn_g"][l].reshape(1, 1, D_MODEL),
        "ffn_w_up_g": p["ffn_w_up"][l][:, :D_FF].astype(BF16), "ffn_w_up_v": p["ffn_w_up"][l][:, D_FF:].astype(BF16),
        "ffn_conv_w": p["ffn_conv_w"][l], "ffn_conv_b": row(p["ffn_conv_b"][l]),
        "ffn_w_down": p["ffn_w_down"][l].astype(BF16),
    }


def kernel(x_prompt, x_sample, c_prompt, c_sample, state_lru_h, state_lru_conv, state_gla, state_ffn_conv, ada_w, ada_b, norm_mix_g, norm_ffn_g, w_in, lru_conv_w, lru_conv_b, lru_wa, lru_ba, lru_wx, lru_bx, lru_lambda, gla_w_lr, gla_b_lr, gla_norm_g, w_branch_a, w_branch_b, w_out, ffn_w_up, ffn_conv_w, ffn_conv_b, ffn_w_down, final_norm_g):
    p = dict(w_in=w_in, lru_conv_w=lru_conv_w, lru_conv_b=lru_conv_b, lru_wa=lru_wa, lru_ba=lru_ba, lru_wx=lru_wx,
             lru_bx=lru_bx, lru_lambda=lru_lambda, gla_w_lr=gla_w_lr, gla_b_lr=gla_b_lr, gla_norm_g=gla_norm_g,
             w_branch_a=w_branch_a, w_branch_b=w_branch_b, w_out=w_out, norm_ffn_g=norm_ffn_g, ffn_w_up=ffn_w_up,
             ffn_conv_w=ffn_conv_w, ffn_conv_b=ffn_conv_b, ffn_w_down=ffn_w_down)
    bp, tp, _ = x_prompt.shape
    bs, ts, _ = x_sample.shape
    lay_p = _Layout(True, (bp, tp), (1, 512))
    lay_s = _Layout(False, (ts, bs), (ts, 64))

    mod_all = _ada_mod(jnp.concatenate([c_prompt, c_sample], axis=0), ada_w, ada_b)
    mods_p = [mod_all[l, :bp].reshape(bp, 1, 6 * D_MODEL) for l in range(DEPTH)]
    mods_s = [mod_all[l, bp:].reshape(1, bs, 6 * D_MODEL) for l in range(DEPTH)]
    zeros_mod_p = jnp.zeros_like(mods_p[0])
    zeros_mod_s = jnp.zeros_like(mods_s[0])

    xp = x_prompt
    xs = jnp.transpose(x_sample, (1, 0, 2))
    hp = _norm_mod(lay_p, xp, norm_mix_g[0], mods_p[0], 0, 1)
    hs = _norm_mod(lay_s, xs, norm_mix_g[0], mods_s[0], 0, 1)

    outs = {k: [] for k in ("hp", "hs", "cp", "cs", "sp", "ss", "fp", "fs")}
    for l in range(DEPTH):
        w = _layer_weights(l, p)
        last = l == DEPTH - 1
        if last:
            gn, mnp, mns, ndt = final_norm_g, zeros_mod_p, zeros_mod_s, F32
        else:
            gn, mnp, mns, ndt = norm_mix_g[l + 1], mods_p[l + 1], mods_s[l + 1], BF16
        gn = gn.reshape(1, 1, D_MODEL)

        h_tm = jnp.transpose(hp, (1, 0, 2)).reshape(tp * bp, D_MODEL)
        z_lru = _matmul(h_tm, w["w_lru"], 1536).reshape(tp, bp, N_LRU)
        z_rest = _matmul(hp.reshape(bp * tp, D_MODEL), w["w_rest"], 1792).reshape(bp, tp, N_REST)
        ya_tm, conv_n, h_last = _lru(z_lru, jnp.zeros((LRU_CONV - 1, bp, LRU_WIDTH), F32),
                                     jnp.zeros((bp, LRU_WIDTH), F32), w, tt=32, bb=bp, reset_first=True)
        yb, s_fin = _gla(z_rest, jnp.zeros((bp, GLA_HEADS, GLA_DK, GLA_DV), F32), w,
                         bb=1, tt=512, out_dtype=BF16, mm=BF16)
        xp, h2 = _merge(lay_p, jnp.transpose(ya_tm, (1, 0, 2)), yb, z_rest, xp, mods_p[l], w)
        xp, hp, ffn_n = _ffn(lay_p, h2, xp, jnp.zeros((bp, FFN_CONV - 1, D_FF), F32), mods_p[l], mnp, 0, 1, gn, w, ndt)
        outs["hp"].append(h_last)
        outs["cp"].append(jnp.transpose(conv_n, (1, 0, 2)))
        outs["sp"].append(s_fin)
        outs["fp"].append(ffn_n)

        hs2d = hs.reshape(ts * bs, D_MODEL)
        z_lru = _matmul(hs2d, w["w_lru"], 1536).reshape(ts, bs, N_LRU)
        z_rest = _matmul(hs2d, w["w_rest"], 1792).reshape(ts, bs, N_REST)
        ya, conv_n, h_last = _lru(z_lru, jnp.transpose(state_lru_conv[l], (1, 0, 2)), state_lru_h[l], w,
                                  tt=ts, bb=32, reset_first=False)
        yb, s_fin = _gla(jnp.transpose(z_rest, (1, 0, 2)), state_gla[l], w, bb=8, tt=ts, out_dtype=F32, mm=F32)
        xs, h2 = _merge(lay_s, ya, jnp.transpose(yb, (1, 0, 2)), z_rest, xs, mods_s[l], w)
        xs, hs, ffn_n = _ffn(lay_s, h2, xs, jnp.transpose(state_ffn_conv[l], (1, 0, 2)), mods_s[l], mns, 0, 1, gn, w,
                             ndt)
        outs["hs"].append(h_last)
        outs["cs"].append(jnp.transpose(conv_n, (1, 0, 2)))
        outs["ss"].append(s_fin)
        outs["fs"].append(jnp.transpose(ffn_n, (1, 0, 2)))

    y_prompt = hp
    y_sample = jnp.transpose(hs, (1, 0, 2))
    st = lambda k: jnp.stack(outs[k])
    return (y_prompt, y_sample, st("hp"), st("hs"), st("cp"), st("cs"), st("sp"), st("ss"), st("fp"), st("fs"))
```
